```python
import math
import jax, jax.numpy as jnp
from jax import lax
import numpy as np

D_MODEL = 2048
BATCH = 1
SEQ = 8192
DEPTH = 1

POOL_WINDOWS = (2, 4, 8, 16)
POOL_GROUPS = len(POOL_WINDOWS)
POOL_GROUP_WIDTH = D_MODEL // 8
POOL_WIDTH = POOL_GROUPS * POOL_GROUP_WIDTH

ATTN_GROUPS = ((128, 1), (512, 4), (2048, 16))
HEADS_PER_GROUP = 4
N_ATTN_HEADS = HEADS_PER_GROUP * len(ATTN_GROUPS)
HEAD_DIM = 128
ATTN_WIDTH = N_ATTN_HEADS * HEAD_DIM
ATTN_OUT_WIDTH = HEADS_PER_GROUP * HEAD_DIM

N_BRANCHES = 2
IN_WIDTH = POOL_WIDTH + 3 * ATTN_WIDTH + N_BRANCHES * D_MODEL

D_FF = 5632
CONV_WIDTH = 3

RMS_EPS = 1e-6

kernel_name = "hybrid_pool_dilated_alibi_convffn_block"


def alibi_slopes(n_heads):
    return np.array([2.0 ** (-8.0 * (h + 1) / n_heads) for h in range(n_heads)], dtype=np.float32)


def rms_norm(x, g):
    xf = x.astype(jnp.float32)
    y = xf * lax.rsqrt(jnp.mean(xf * xf, axis=-1, keepdims=True) + RMS_EPS) * g.astype(jnp.float32)
    return y.astype(x.dtype)


def pool_mixer(u, w_lin, scale):
    B, S, _ = u.shape
    uf = u.astype(jnp.float32).reshape(B, S, POOL_GROUPS, POOL_GROUP_WIDTH)
    t = jnp.arange(S)
    outs = []
    for gi, w in enumerate(POOL_WINDOWS):
        ug = uf[:, :, gi]
        cs = jnp.cumsum(ug, axis=1)
        lag = jnp.pad(cs, ((0, 0), (w, 0), (0, 0)))[:, :S]
        cnt = jnp.minimum(t + 1, w).astype(jnp.float32)[None, :, None]
        outs.append((cs - lag) / cnt - ug)
    pooled = jnp.stack(outs, axis=2)
    y = jnp.einsum('bsgc,gce->bsge', pooled, w_lin.astype(jnp.float32))
    y = y.reshape(B, S, POOL_WIDTH) * scale.astype(jnp.float32)
    return y.astype(u.dtype)


def dilated_group_attention(q, k, v, slopes, window, dilation):
    B, S, H, Dh = q.shape
    span = window // dilation
    L = S // dilation
    nb = -(-L // span)
    Lp = nb * span
    N = B * dilation

    def to_sub(a):
        a = a.reshape(B, L, dilation, H, Dh).transpose(0, 2, 1, 3, 4).reshape(N, L, H, Dh)
        a = jnp.pad(a, ((0, 0), (0, Lp - L), (0, 0), (0, 0)))
        return a.reshape(N, nb, span, H, Dh)

    def with_prev(a):
        prev = jnp.pad(a, ((0, 0), (1, 0), (0, 0), (0, 0), (0, 0)))[:, :nb]
        return jnp.concatenate([prev, a], axis=2)

    qb = to_sub(q)
    kk = with_prev(to_sub(k))
    vv = with_prev(to_sub(v))

    s = jnp.einsum('nbqhd,nbkhd->nbhqk', qb, kk, preferred_element_type=jnp.float32) * (Dh ** -0.5)
    qi = jnp.arange(span)[:, None] + span
    ki = jnp.arange(2 * span)[None, :]
    j = qi - ki
    key_abs = (jnp.arange(nb) * span)[:, None] - span + jnp.arange(2 * span)[None, :]
    valid = ((j >= 0) & (j <= span))[None] & (key_abs >= 0)[:, None, :]
    bias = -slopes[:, None, None] * (j * dilation).astype(jnp.float32)[None]
    s = jnp.where(valid[None, :, None], s + bias[None, None], -jnp.inf)
    m = jnp.max(s, axis=-1, keepdims=True)
    p = jnp.exp(s - m)
    l = jnp.sum(p, axis=-1, keepdims=True)
    o = jnp.einsum('nbhqk,nbkhd->nbqhd', p, vv.astype(jnp.float32)) / jnp.swapaxes(l, 2, 3)
    lse = jnp.swapaxes((m + jnp.log(l))[..., 0], 2, 3)

    o = o.reshape(N, Lp, H, Dh)[:, :L].reshape(B, dilation, L, H, Dh).transpose(0, 2, 1, 3, 4).reshape(B, S, H, Dh)
    lse = lse.reshape(N, Lp, H)[:, :L].reshape(B, dilation, L, H).transpose(0, 2, 1, 3).reshape(B, S, H)
    return o, lse


def dilated_attention_mixer(q, k, v):
    B, S = q.shape[:2]
    slopes = jnp.asarray(alibi_slopes(N_ATTN_HEADS))
    outs, lses = [], []
    for gi, (window, dilation) in enumerate(ATTN_GROUPS):
        hs = slice(gi * HEADS_PER_GROUP, (gi + 1) * HEADS_PER_GROUP)
        o, lse = dilated_group_attention(q[:, :, hs], k[:, :, hs], v[:, :, hs], slopes[hs], window, dilation)
        outs.append(o)
        lses.append(lse)
    wts = jax.nn.softmax(jnp.stack(lses, axis=0), axis=0)
    y = jnp.sum(wts[..., None] * jnp.stack(outs, axis=0), axis=0)
    return y.reshape(B, S, ATTN_OUT_WIDTH).astype(q.dtype)


def causal_dwconv(u, w, b):
    S = u.shape[1]
    up = jnp.pad(u, ((0, 0), (CONV_WIDTH - 1, 0), (0, 0)))
    y = b
    for i in range(CONV_WIDTH):
        y = y + w[i] * up[:, i:i + S]
    return y


def setup_inputs(seed: int = 0) -> dict:
    key = jax.random.key(seed)
    ks = jax.random.split(key, 16)
    f32 = jnp.float32
    nrm = lambda k, shape, fan: jax.random.normal(k, shape, f32) * (fan ** -0.5)
    return {
        "x": jax.random.normal(ks[0], (BATCH, SEQ, D_MODEL), f32),
        "g_mix": 1.0 + 0.02 * jax.random.normal(ks[1], (DEPTH, D_MODEL), f32),
        "w_in": nrm(ks[2], (DEPTH, D_MODEL, IN_WIDTH), D_MODEL),
        "b_gate": 0.1 * jax.random.normal(ks[3], (DEPTH, N_BRANCHES * D_MODEL), f32),
        "w_pool_lin": nrm(ks[4], (DEPTH, POOL_GROUPS, POOL_GROUP_WIDTH, POOL_GROUP_WIDTH), POOL_GROUP_WIDTH),
        "pool_scale": 1.0 + 0.02 * jax.random.normal(ks[5], (DEPTH, POOL_WIDTH), f32),
        "w_pool_out": nrm(ks[6], (DEPTH, POOL_WIDTH, D_MODEL), POOL_WIDTH),
        "w_attn_out": nrm(ks[7], (DEPTH, ATTN_OUT_WIDTH, D_MODEL), ATTN_OUT_WIDTH),
        "w_out": nrm(ks[8], (DEPTH, D_MODEL, D_MODEL), D_MODEL),
        "g_ffn": 1.0 + 0.02 * jax.random.normal(ks[9], (DEPTH, D_MODEL), f32),
        "w_up": nrm(ks[10], (DEPTH, D_MODEL, 2 * D_FF), D_MODEL),
        "conv_w": nrm(ks[11], (DEPTH, CONV_WIDTH, 2 * D_FF), CONV_WIDTH),
        "conv_b": 0.02 * jax.random.normal(ks[12], (DEPTH, 2 * D_FF), f32),
        "w_down": nrm(ks[13], (DEPTH, D_FF, D_MODEL), D_FF),
        "g_final": 1.0 + 0.02 * jax.random.normal(ks[14], (D_MODEL,), f32),
    }


def reference(x, g_mix, w_in, b_gate, w_pool_lin, pool_scale, w_pool_out, w_attn_out, w_out,
              g_ffn, w_up, conv_w, conv_b, w_down, g_final):
    B, S, _ = x.shape
    o_q = POOL_WIDTH
    o_k = o_q + ATTN_WIDTH
    o_v = o_k + ATTN_WIDTH
    o_g = o_v + ATTN_WIDTH
    for l in range(DEPTH):
        h = rms_norm(x, g_mix[l])
        proj = h @ w_in[l]
        u = proj[..., :o_q]
        q = proj[..., o_q:o_k].reshape(B, S, N_ATTN_HEADS, HEAD_DIM)
        k = proj[..., o_k:o_v].reshape(B, S, N_ATTN_HEADS, HEAD_DIM)
        v = proj[..., o_v:o_g].reshape(B, S, N_ATTN_HEADS, HEAD_DIM)
        gates = jax.nn.sigmoid(proj[..., o_g:] + b_gate[l]).reshape(B, S, N_BRANCHES, D_MODEL)

        y_pool = pool_mixer(u, w_pool_lin[l], pool_scale[l]) @ w_pool_out[l]
        y_attn = dilated_attention_mixer(q, k, v) @ w_attn_out[l]
        mixed = gates[:, :, 0] * y_pool + gates[:, :, 1] * y_attn
        x = x + mixed @ w_out[l]

        h = rms_norm(x, g_ffn[l])
        up = causal_dwconv(h @ w_up[l], conv_w[l], conv_b[l])
        a, b = up[..., :D_FF], up[..., D_FF:]
        x = x + (jax.nn.gelu(a, approximate=False) * b) @ w_down[l]
    return rms_norm(x, g_final)
```

```python
import functools
import math

import jax
import jax.numpy as jnp
import numpy as np
from jax import lax
from jax.experimental import pallas as pl
from jax.experimental.pallas import tpu as pltpu

F32 = jnp.float32
BF16 = jnp.bfloat16

D_MODEL = 2048
POOL_WINDOWS = (2, 4, 8, 16)
POOL_GROUP_WIDTH = 256
POOL_WIDTH = len(POOL_WINDOWS) * POOL_GROUP_WIDTH
ATTN_GROUPS = ((128, 1), (512, 4), (2048, 16))
HEADS_PER_GROUP = 4
N_ATTN_HEADS = HEADS_PER_GROUP * len(ATTN_GROUPS)
HEAD_DIM = 128
ATTN_WIDTH = N_ATTN_HEADS * HEAD_DIM
GROUP_WIDTH = HEADS_PER_GROUP * HEAD_DIM
QKV_WIDTH = 3 * ATTN_WIDTH
GATE_WIDTH = 2 * D_MODEL
D_FF = 5632
CONV_WIDTH = 3
RMS_EPS = 1e-6

SPAN = 128
POOL_HALO = 16
CONV_HALO = 16

MIB = 1024 * 1024


def _rms_norm(x, g):
    return x * lax.rsqrt(jnp.mean(x * x, axis=-1, keepdims=True) + RMS_EPS) * g


IN_TM = 1024
IN_TN = 512
IN_NU = POOL_WIDTH // IN_TN
IN_NQKV = QKV_WIDTH // IN_TN
IN_NG = GATE_WIDTH // IN_TN


def _in_proj_kernel(x_ref, g_ref, w_ref, b_ref, u_ref, qkv_ref, gate_ref, h_ref):
    j = pl.program_id(1)

    @pl.when(j == 0)
    def _():
        h_ref[...] = _rms_norm(x_ref[...], g_ref[...]).astype(BF16)

    acc = jnp.dot(h_ref[...], w_ref[...], preferred_element_type=F32)

    @pl.when(j < IN_NU)
    def _():
        u_ref[...] = acc

    @pl.when((j >= IN_NU) & (j < IN_NU + IN_NQKV))
    def _():
        qkv_ref[...] = acc.astype(BF16)

    @pl.when(j >= IN_NU + IN_NQKV)
    def _():
        gate_ref[...] = jax.nn.sigmoid(acc + b_ref[...]).astype(BF16)


def _in_proj(x, g_mix, w_in_bf16, b_gate):
    s = x.shape[0]
    n_col = IN_NU + IN_NQKV + IN_NG
    grid = (s // IN_TM, n_col)
    return pl.pallas_call(
        _in_proj_kernel,
        grid=grid,
        in_specs=[
            pl.BlockSpec((IN_TM, D_MODEL), lambda i, j: (i, 0)),
            pl.BlockSpec((1, D_MODEL), lambda i, j: (0, 0)),
            pl.BlockSpec((D_MODEL, IN_TN), lambda i, j: (0, j)),
            pl.BlockSpec((1, IN_TN), lambda i, j: (0, jnp.clip(j - IN_NU - IN_NQKV, 0, IN_NG - 1))),
        ],
        out_specs=[
            pl.BlockSpec((IN_TM, IN_TN), lambda i, j: (i, jnp.minimum(j, IN_NU - 1))),
            pl.BlockSpec((IN_TM, IN_TN), lambda i, j: (i, jnp.clip(j - IN_NU, 0, IN_NQKV - 1))),
            pl.BlockSpec((IN_TM, IN_TN), lambda i, j: (i, jnp.clip(j - IN_NU - IN_NQKV, 0, IN_NG - 1))),
        ],
        out_shape=[
            jax.ShapeDtypeStruct((s, POOL_WIDTH), F32),
            jax.ShapeDtypeStruct((s, QKV_WIDTH), BF16),
            jax.ShapeDtypeStruct((s, GATE_WIDTH), BF16),
        ],
        scratch_shapes=[pltpu.VMEM((IN_TM, D_MODEL), BF16)],
        compiler_params=pltpu.CompilerParams(
            dimension_semantics=("arbitrary", "arbitrary"),
            vmem_limit_bytes=48 * MIB,
        ),
        name="in_proj",
    )(x, g_mix, w_in_bf16, b_gate)


ATTN_TQ = 512
ATTN_NSUB = ATTN_TQ // SPAN


def _alibi_bias_table(group_index, dilation):
    slopes = np.array([2.0 ** (-8.0 * (h + 1) / N_ATTN_HEADS) for h in range(N_ATTN_HEADS)], dtype=np.float32)
    slopes = slopes[group_index * HEADS_PER_GROUP:(group_index + 1) * HEADS_PER_GROUP]
    qi = np.arange(SPAN)[:, None] + SPAN
    ki = np.arange(2 * SPAN)[None, :]
    j = qi - ki
    valid = (j >= 0) & (j <= SPAN)
    bias = -slopes[:, None, None] * (j * dilation).astype(np.float32)[None]
    return np.where(valid[None], bias, -np.inf).astype(np.float32)


def _attn_kernel(q_ref, kc_ref, kp_ref, vc_ref, vp_ref, bias_ref, o_ref, lse_ref, kbuf, vbuf):
    b = pl.program_id(1)
    kbuf[0:SPAN, :] = kp_ref[...]
    kbuf[SPAN:, :] = kc_ref[...]
    vbuf[0:SPAN, :] = vp_ref[...]
    vbuf[SPAN:, :] = vc_ref[...]

    scale = HEAD_DIM ** -0.5
    lane = lax.broadcasted_iota(jnp.int32, (SPAN, HEAD_DIM), 1)
    col = lax.broadcasted_iota(jnp.int32, (SPAN, 2 * SPAN), 1)
    before_start = (b == 0) & (col < SPAN)

    for sub in range(ATTN_NSUB):
        rows = slice(sub * SPAN, (sub + 1) * SPAN)
        krows = slice(sub * SPAN, (sub + 2) * SPAN)
        lse_tile = jnp.zeros((SPAN, HEAD_DIM), F32)
        for h in range(HEADS_PER_GROUP):
            cols = slice(h * HEAD_DIM, (h + 1) * HEAD_DIM)
            q = q_ref[rows, cols]
            k = kbuf[krows, cols]
            v = vbuf[krows, cols]
            s = lax.dot_general(q, k, (((1,), (1,)), ((), ())), preferred_element_type=F32)
            s = s * scale + bias_ref[h]
            if sub == 0:
                s = jnp.where(before_start, -jnp.inf, s)
            m = jnp.max(s, axis=-1, keepdims=True)
            p = jnp.exp(s - m)
            l = jnp.sum(p, axis=-1, keepdims=True)
            o = jnp.dot(p.astype(BF16), v, preferred_element_type=F32) / l
            o_ref[rows, cols] = o
            lse_tile = jnp.where(lane == h, m + jnp.log(l), lse_tile)
        lse_ref[rows, :] = lse_tile


def _attention_group(qkv, group_index, dilation):
    s = qkv.shape[0]
    d = dilation
    L = s // d
    tq = min(ATTN_TQ, L)
    assert tq == ATTN_TQ and L % tq == 0
    nblk = QKV_WIDTH // GROUP_WIDTH
    qv = qkv.reshape(L, d * QKV_WIDTH)
    q_blk = group_index
    k_blk = ATTN_WIDTH // GROUP_WIDTH + group_index
    v_blk = 2 * ATTN_WIDTH // GROUP_WIDTH + group_index
    bias = jnp.asarray(_alibi_bias_table(group_index, d))

    def cur(blk):
        return pl.BlockSpec((tq, GROUP_WIDTH), lambda r, b: (b, r * nblk + blk))

    def prev(blk):
        return pl.BlockSpec((SPAN, GROUP_WIDTH),
                            lambda r, b: (jnp.maximum(b * ATTN_NSUB - 1, 0), r * nblk + blk))

    o, lse = pl.pallas_call(
        _attn_kernel,
        grid=(d, L // tq),
        in_specs=[cur(q_blk), cur(k_blk), prev(k_blk), cur(v_blk), prev(v_blk),
                  pl.BlockSpec((HEADS_PER_GROUP, SPAN, 2 * SPAN), lambda r, b: (0, 0, 0))],
        out_specs=[
            pl.BlockSpec((tq, GROUP_WIDTH), lambda r, b: (b, r)),
            pl.BlockSpec((tq, HEAD_DIM), lambda r, b: (b, r)),
        ],
        out_shape=[
            jax.ShapeDtypeStruct((L, d * GROUP_WIDTH), F32),
            jax.ShapeDtypeStruct((L, d * HEAD_DIM), F32),
        ],
        scratch_shapes=[pltpu.VMEM((SPAN + tq, GROUP_WIDTH), BF16),
                        pltpu.VMEM((SPAN + tq, GROUP_WIDTH), BF16)],
        compiler_params=pltpu.CompilerParams(
            dimension_semantics=("arbitrary", "arbitrary"),
            vmem_limit_bytes=32 * MIB,
        ),
        name=f"attn_d{d}",
    )(qv, qv, qv, qv, qv, bias)
    return o.reshape(s, GROUP_WIDTH), lse.reshape(s, HEAD_DIM)


MIX_TM = 256


def _mix_kernel(u_ref, uh_ref, o0_ref, o1_ref, o2_ref, l0_ref, l1_ref, l2_ref, gate_ref, x_ref,
                wlin_ref, pscale_ref, wpo_ref, wao_ref, wout_ref, gffn_ref,
                x1_ref, h2_ref, ubuf):
    i = pl.program_id(0)
    tm = MIX_TM

    ubuf[0:POOL_HALO, :] = jnp.where(i == 0, 0.0, uh_ref[...])
    ubuf[POOL_HALO:, :] = u_ref[...]
    t = i * tm + lax.broadcasted_iota(jnp.int32, (tm, 1), 0)
    ys = []
    for gi, w in enumerate(POOL_WINDOWS):
        cols = slice(gi * POOL_GROUP_WIDTH, (gi + 1) * POOL_GROUP_WIDTH)
        ug = ubuf[POOL_HALO:POOL_HALO + tm, cols]
        win = ug
        for k in range(1, w):
            win = win + ubuf[POOL_HALO - k:POOL_HALO - k + tm, cols]
        cnt = jnp.minimum(t + 1, w).astype(F32)
        pooled = win / cnt - ug
        y = jnp.dot(pooled.astype(BF16), wlin_ref[gi], preferred_element_type=F32)
        ys.append(y)
    y = jnp.concatenate(ys, axis=1) * pscale_ref[...]
    y_pool = jnp.dot(y.astype(BF16), wpo_ref[...], preferred_element_type=F32)

    l0, l1, l2 = l0_ref[...], l1_ref[...], l2_ref[...]
    m = jnp.maximum(jnp.maximum(l0, l1), l2)
    e0, e1, e2 = jnp.exp(l0 - m), jnp.exp(l1 - m), jnp.exp(l2 - m)
    den = e0 + e1 + e2
    w0, w1, w2 = e0 / den, e1 / den, e2 / den
    parts = []
    for h in range(HEADS_PER_GROUP):
        cols = slice(h * HEAD_DIM, (h + 1) * HEAD_DIM)
        parts.append(w0[:, h:h + 1] * o0_ref[:, cols]
                     + w1[:, h:h + 1] * o1_ref[:, cols]
                     + w2[:, h:h + 1] * o2_ref[:, cols])
    y_att = jnp.concatenate(parts, axis=1)
    y_attn = jnp.dot(y_att.astype(BF16), wao_ref[...], preferred_element_type=F32)

    g_pool = gate_ref[:, 0:D_MODEL].astype(F32)
    g_attn = gate_ref[:, D_MODEL:].astype(F32)
    mixed = g_pool * y_pool + g_attn * y_attn
    x1 = x_ref[...] + jnp.dot(mixed.astype(BF16), wout_ref[...], preferred_element_type=F32)
    x1_ref[...] = x1
    h2_ref[...] = _rms_norm(x1, gffn_ref[...]).astype(BF16)


def _mix(u, os_, lses, gates, x, wlin, pscale, wpo, wao, wout, g_ffn):
    s = x.shape[0]
    tm = MIX_TM
    row = lambda w: pl.BlockSpec((tm, w), lambda i: (i, 0))
    const2 = lambda a: pl.BlockSpec(a.shape, lambda i: (0, 0))
    halo = pl.BlockSpec((POOL_HALO, POOL_WIDTH),
                        lambda i: (jnp.maximum(i * (tm // POOL_HALO) - 1, 0), 0))
    return pl.pallas_call(
        _mix_kernel,
        grid=(s // tm,),
        in_specs=[row(POOL_WIDTH), halo,
                  row(GROUP_WIDTH), row(GROUP_WIDTH), row(GROUP_WIDTH),
                  row(HEAD_DIM), row(HEAD_DIM), row(HEAD_DIM),
                  row(GATE_WIDTH), row(D_MODEL),
                  pl.BlockSpec(wlin.shape, lambda i: (0, 0, 0)),
                  const2(pscale), const2(wpo), const2(wao), const2(wout), const2(g_ffn)],
        out_specs=[row(D_MODEL), row(D_MODEL)],
        out_shape=[jax.ShapeDtypeStruct((s, D_MODEL), F32),
                   jax.ShapeDtypeStruct((s, D_MODEL), BF16)],
        scratch_shapes=[pltpu.VMEM((POOL_HALO + tm, POOL_WIDTH), F32)],
        compiler_params=pltpu.CompilerParams(
            dimension_semantics=("arbitrary",),
            vmem_limit_bytes=56 * MIB,
        ),
        name="mix",
    )(u, u, *os_, *lses, gates, x, wlin, pscale, wpo, wao, wout, g_ffn)


FFN_TM = 512
FFN_TF = 512
FFN_NJ = D_FF // FFN_TF


def _gelu_exact(a):
    return 0.5 * a * (1.0 + lax.erf(a * np.float32(math.sqrt(0.5))))


def _causal_conv(up, halo_up, cw_ref, cb_ref, sbuf):
    tm = up.shape[0]
    sbuf[0:CONV_HALO, :] = halo_up
    sbuf[CONV_HALO:, :] = up
    y = cb_ref[...] + cw_ref[0:1, :] * sbuf[CONV_HALO - 2:CONV_HALO - 2 + tm, :]
    y = y + cw_ref[1:2, :] * sbuf[CONV_HALO - 1:CONV_HALO - 1 + tm, :]
    return y + cw_ref[2:3, :] * up


def _ffn_kernel(h_ref, hh_ref, wa_ref, wb_ref, cwa_ref, cwb_ref, cba_ref, cbb_ref, wd_ref,
                x1_ref, gfin_ref, out_ref, sbuf):
    i = pl.program_id(0)
    j = pl.program_id(1)
    h = h_ref[...]
    hh = hh_ref[...]
    first = i == 0

    def branch(w_ref, cw_ref, cb_ref):
        up = jnp.dot(h, w_ref[...], preferred_element_type=F32)
        halo = jnp.dot(hh, w_ref[...], preferred_element_type=F32)
        halo = jnp.where(first, 0.0, halo)
        return _causal_conv(up, halo, cw_ref, cb_ref, sbuf)

    a = branch(wa_ref, cwa_ref, cba_ref)
    b = branch(wb_ref, cwb_ref, cbb_ref)
    act = (_gelu_exact(a) * b).astype(BF16)
    part = jnp.dot(act, wd_ref[...], preferred_element_type=F32)

    @pl.when(j == 0)
    def _():
        out_ref[...] = x1_ref[...] + part

    @pl.when((j > 0) & (j < FFN_NJ - 1))
    def _():
        out_ref[...] += part

    @pl.when(j == FFN_NJ - 1)
    def _():
        out_ref[...] = _rms_norm(out_ref[...] + part, gfin_ref[...])


def _ffn(h2, x1, w_up_bf16, conv_w, conv_b, w_down_bf16, g_final):
    s = h2.shape[0]
    tm, tf, nj = FFN_TM, FFN_TF, FFN_NJ
    return pl.pallas_call(
        _ffn_kernel,
        grid=(s // tm, nj),
        in_specs=[
            pl.BlockSpec((tm, D_MODEL), lambda i, j: (i, 0)),
            pl.BlockSpec((CONV_HALO, D_MODEL), lambda i, j: (jnp.maximum(i * (tm // CONV_HALO) - 1, 0), 0)),
            pl.BlockSpec((D_MODEL, tf), lambda i, j: (0, j)),
            pl.BlockSpec((D_MODEL, tf), lambda i, j: (0, j + nj)),
            pl.BlockSpec((CONV_WIDTH, tf), lambda i, j: (0, j)),
            pl.BlockSpec((CONV_WIDTH, tf), lambda i, j: (0, j + nj)),
            pl.BlockSpec((1, tf), lambda i, j: (0, j)),
            pl.BlockSpec((1, tf), lambda i, j: (0, j + nj)),
            pl.BlockSpec((tf, D_MODEL), lambda i, j: (j, 0)),
            pl.BlockSpec((tm, D_MODEL), lambda i, j: (i, 0)),
            pl.BlockSpec((1, D_MODEL), lambda i, j: (0, 0)),
        ],
        out_specs=pl.BlockSpec((tm, D_MODEL), lambda i, j: (i, 0)),
        out_shape=jax.ShapeDtypeStruct((s, D_MODEL), F32),
        scratch_shapes=[pltpu.VMEM((CONV_HALO + tm, tf), F32)],
        compiler_params=pltpu.CompilerParams(
            dimension_semantics=("arbitrary", "arbitrary"),
            vmem_limit_bytes=56 * MIB,
        ),
        name="ffn",
    )(h2, h2, w_up_bf16, w_up_bf16, conv_w, conv_w, conv_b, conv_b, w_down_bf16, x1, g_final)


def kernel(x, g_mix, w_in, b_gate, w_pool_lin, pool_scale, w_pool_out, w_attn_out, w_out,
           g_ffn, w_up, conv_w, conv_b, w_down, g_final):
    batch, s, _ = x.shape
    assert batch == 1
    depth = g_mix.shape[0]
    xs = x[0]
    for l in range(depth):
        u, qkv, gates = _in_proj(xs, g_mix[l][None], w_in[l].astype(BF16), b_gate[l][None])
        os_, lses = [], []
        for gi, (_, dilation) in enumerate(ATTN_GROUPS):
            o, lse = _attention_group(qkv, gi, dilation)
            os_.append(o)
            lses.append(lse)
        x1, h2 = _mix(u, os_, lses, gates, xs,
                      w_pool_lin[l].astype(BF16), pool_scale[l][None],
                      w_pool_out[l].astype(BF16), w_attn_out[l].astype(BF16),
                      w_out[l].astype(BF16), g_ffn[l][None])
        last = l == depth - 1
        assert last, "stacked layers need the un-normalised residual stream"
        xs = _ffn(h2, x1, w_up[l].astype(BF16), conv_w[l], conv_b[l][None],
                  w_down[l].astype(BF16), g_final[None])
    return xs[None]
```

```python
import functools
import math

import jax
import jax.numpy as jnp
import numpy as np
from jax import lax
from jax.experimental import pallas as pl
from jax.experimental.pallas import tpu as pltpu

F32 = jnp.float32
BF16 = jnp.bfloat16

D_MODEL = 2048
POOL_WINDOWS = (2, 4, 8, 16)
POOL_GROUP_WIDTH = 256
POOL_WIDTH = len(POOL_WINDOWS) * POOL_GROUP_WIDTH
ATTN_GROUPS = ((128, 1), (512, 4), (2048, 16))
DILATIONS = tuple(d for _, d in ATTN_GROUPS)
N_GROUPS = len(ATTN_GROUPS)
HEADS_PER_GROUP = 4
N_ATTN_HEADS = HEADS_PER_GROUP * N_GROUPS
HEAD_DIM = 128
ATTN_WIDTH = N_ATTN_HEADS * HEAD_DIM
GROUP_WIDTH = HEADS_PER_GROUP * HEAD_DIM
GROUP_QKV_WIDTH = 3 * GROUP_WIDTH
GATE_WIDTH = 2 * D_MODEL
D_FF = 5632
CONV_WIDTH = 3
RMS_EPS = 1e-6

SPAN = 128
POOL_HALO = 16
LANES = 128
SUBLANES = 8

MIB = 1024 * 1024


def _rms_norm(x, g):
    return x * lax.rsqrt(jnp.mean(x * x, axis=-1, keepdims=True) + RMS_EPS) * g


IN_TM = 1024
IN_TN = 512
IN_NU = POOL_WIDTH // IN_TN
IN_NGRP = GROUP_QKV_WIDTH // IN_TN
IN_NG = GATE_WIDTH // IN_TN
IN_GATE0 = IN_NU + N_GROUPS * IN_NGRP


def _in_proj_kernel(x_ref, g_ref, w_ref, b_ref, u_ref, qkv0_ref, qkv1_ref, qkv2_ref, gate_ref,
                    h_ref, slab_ref):
    j = pl.program_id(1)

    @pl.when(j == 0)
    def _():
        h_ref[...] = _rms_norm(x_ref[...], g_ref[...]).astype(BF16)

    acc = jnp.dot(h_ref[...], w_ref[...], preferred_element_type=F32)

    @pl.when(j < IN_NU)
    def _():
        u_ref[...] = acc

    for gi, (d, out_ref) in enumerate(zip(DILATIONS, (qkv0_ref, qkv1_ref, qkv2_ref))):
        lo = IN_NU + gi * IN_NGRP

        @pl.when((j >= lo) & (j < lo + IN_NGRP))
        def _(d=d, out_ref=out_ref):
            if d == 1:
                out_ref[0] = acc.astype(BF16)
                return
            for c in range(IN_TN // LANES):
                slab_ref[c] = acc[:, c * LANES:(c + 1) * LANES]
            for r in range(d):
                for c in range(IN_TN // LANES):
                    rows = slab_ref[c, pl.ds(r, IN_TM // d, stride=d), :]
                    out_ref[r, :, c * LANES:(c + 1) * LANES] = rows.astype(BF16)

    @pl.when(j >= IN_GATE0)
    def _():
        gate_ref[...] = jax.nn.sigmoid(acc + b_ref[...]).astype(BF16)


def _in_proj(x, g_mix, w_in_bf16, b_gate):
    s = x.shape[0]
    grid = (s // IN_TM, IN_GATE0 + IN_NG)

    def qkv_spec(gi, d):
        lo = IN_NU + gi * IN_NGRP
        return pl.BlockSpec((d, IN_TM // d, IN_TN), lambda i, j: (0, i, jnp.clip(j - lo, 0, IN_NGRP - 1)))

    gate_col = lambda i, j: jnp.clip(j - IN_GATE0, 0, IN_NG - 1)
    return pl.pallas_call(
        _in_proj_kernel,
        grid=grid,
        in_specs=[
            pl.BlockSpec((IN_TM, D_MODEL), lambda i, j: (i, 0)),
            pl.BlockSpec((1, D_MODEL), lambda i, j: (0, 0)),
            pl.BlockSpec((D_MODEL, IN_TN), lambda i, j: (0, j)),
            pl.BlockSpec((1, IN_TN), lambda i, j: (0, gate_col(i, j))),
        ],
        out_specs=[
            pl.BlockSpec((IN_TM, IN_TN), lambda i, j: (i, jnp.minimum(j, IN_NU - 1))),
            *[qkv_spec(gi, d) for gi, d in enumerate(DILATIONS)],
            pl.BlockSpec((IN_TM, IN_TN), lambda i, j: (i, gate_col(i, j))),
        ],
        out_shape=[
            jax.ShapeDtypeStruct((s, POOL_WIDTH), F32),
            *[jax.ShapeDtypeStruct((d, s // d, GROUP_QKV_WIDTH), BF16) for d in DILATIONS],
            jax.ShapeDtypeStruct((s, GATE_WIDTH), BF16),
        ],
        scratch_shapes=[pltpu.VMEM((IN_TM, D_MODEL), BF16),
                        pltpu.VMEM((IN_TN // LANES, IN_TM, LANES), F32)],
        compiler_params=pltpu.CompilerParams(
            dimension_semantics=("arbitrary", "arbitrary"),
            vmem_limit_bytes=48 * MIB,
        ),
        name="in_proj",
    )(x, g_mix, w_in_bf16, b_gate)


def _in_proj_weight(w_in):
    o_q = POOL_WIDTH
    o_k = o_q + ATTN_WIDTH
    o_v = o_k + ATTN_WIDTH
    o_g = o_v + ATTN_WIDTH
    parts = [w_in[:, :o_q]]
    for gi in range(N_GROUPS):
        for base in (o_q, o_k, o_v):
            parts.append(w_in[:, base + gi * GROUP_WIDTH: base + (gi + 1) * GROUP_WIDTH])
    parts.append(w_in[:, o_g:])
    return jnp.concatenate(parts, axis=1).astype(BF16)


ATTN_TQ = 512
ATTN_NSUB = ATTN_TQ // SPAN


def _alibi_bias_table(group_index, dilation):
    slopes = np.array([2.0 ** (-8.0 * (h + 1) / N_ATTN_HEADS) for h in range(N_ATTN_HEADS)], dtype=np.float32)
    slopes = slopes[group_index * HEADS_PER_GROUP:(group_index + 1) * HEADS_PER_GROUP]
    qi = np.arange(SPAN)[:, None] + SPAN
    ki = np.arange(2 * SPAN)[None, :]
    j = qi - ki
    valid = (j >= 0) & (j <= SPAN)
    bias = -slopes[:, None, None] * (j * dilation).astype(np.float32)[None]
    return np.where(valid[None], bias, -np.inf).astype(np.float32)


def _attn_kernel(q_ref, kc_ref, kp_ref, vc_ref, vp_ref, bias_ref, o_ref, lse_ref, kbuf, vbuf):
    b = pl.program_id(1)
    kbuf[0:SPAN, :] = kp_ref[...]
    kbuf[SPAN:, :] = kc_ref[...]
    vbuf[0:SPAN, :] = vp_ref[...]
    vbuf[SPAN:, :] = vc_ref[...]

    scale = HEAD_DIM ** -0.5
    lane = lax.broadcasted_iota(jnp.int32, (SPAN, HEAD_DIM), 1)
    col = lax.broadcasted_iota(jnp.int32, (SPAN, 2 * SPAN), 1)
    before_start = (b == 0) & (col < SPAN)

    for sub in range(ATTN_NSUB):
        rows = slice(sub * SPAN, (sub + 1) * SPAN)
        krows = slice(sub * SPAN, (sub + 2) * SPAN)
        lse_tile = jnp.zeros((SPAN, HEAD_DIM), F32)
        for h in range(HEADS_PER_GROUP):
            cols = slice(h * HEAD_DIM, (h + 1) * HEAD_DIM)
            q = q_ref[rows, cols]
            k = kbuf[krows, cols]
            v = vbuf[krows, cols]
            s = lax.dot_general(q, k, (((1,), (1,)), ((), ())), preferred_element_type=F32)
            s = s * scale + bias_ref[h]
            if sub == 0:
                s = jnp.where(before_start, -jnp.inf, s)
            m = jnp.max(s, axis=-1, keepdims=True)
            p = jnp.exp(s - m)
            l = jnp.sum(p, axis=-1, keepdims=True)
            o = jnp.dot(p.astype(BF16), v, preferred_element_type=F32) / l
            o_ref[rows, cols] = o
            lse_tile = jnp.where(lane == h, m + jnp.log(l), lse_tile)
        lse_ref[rows, :] = lse_tile


def _attention_group(qkv_g, group_index):
    d, L, _ = qkv_g.shape
    tq = ATTN_TQ
    assert L % tq == 0
    bias = jnp.asarray(_alibi_bias_table(group_index, d))

    def cur(blk):
        return pl.BlockSpec((None, tq, GROUP_WIDTH), lambda r, b: (r, b, blk))

    def prev(blk):
        return pl.BlockSpec((None, SPAN, GROUP_WIDTH),
                            lambda r, b: (r, jnp.maximum(b * ATTN_NSUB - 1, 0), blk))

    return pl.pallas_call(
        _attn_kernel,
        grid=(d, L // tq),
        in_specs=[cur(0), cur(1), prev(1), cur(2), prev(2),
                  pl.BlockSpec((HEADS_PER_GROUP, SPAN, 2 * SPAN), lambda r, b: (0, 0, 0))],
        out_specs=[
            pl.BlockSpec((None, tq, GROUP_WIDTH), lambda r, b: (r, b, 0)),
            pl.BlockSpec((None, tq, HEAD_DIM), lambda r, b: (r, b, 0)),
        ],
        out_shape=[
            jax.ShapeDtypeStruct((d, L, GROUP_WIDTH), F32),
            jax.ShapeDtypeStruct((d, L, HEAD_DIM), F32),
        ],
        scratch_shapes=[pltpu.VMEM((SPAN + tq, GROUP_WIDTH), BF16),
                        pltpu.VMEM((SPAN + tq, GROUP_WIDTH), BF16)],
        compiler_params=pltpu.CompilerParams(
            dimension_semantics=("arbitrary", "arbitrary"),
            vmem_limit_bytes=32 * MIB,
        ),
        name=f"attn_d{d}",
    )(qkv_g, qkv_g, qkv_g, qkv_g, qkv_g, bias)


MIX_TM = 256
MIX_NSLAB = HEADS_PER_GROUP + 1


def _mix_kernel(u_ref, uh_ref, o0_ref, o1_ref, o2_ref, l0_ref, l1_ref, l2_ref, gate_ref, x_ref,
                wlin_ref, pscale_ref, wpo_ref, wao_ref, wout_ref, gffn_ref,
                x1_ref, h2_ref, ubuf, nat1, nat2):
    i = pl.program_id(0)
    tm = MIX_TM

    ubuf[0:POOL_HALO, :] = jnp.where(i == 0, 0.0, uh_ref[...])
    ubuf[POOL_HALO:, :] = u_ref[...]
    t = i * tm + lax.broadcasted_iota(jnp.int32, (tm, 1), 0)
    ys = []
    for gi, w in enumerate(POOL_WINDOWS):
        cols = slice(gi * POOL_GROUP_WIDTH, (gi + 1) * POOL_GROUP_WIDTH)
        ug = ubuf[POOL_HALO:POOL_HALO + tm, cols]
        win = ug
        for k in range(1, w):
            win = win + ubuf[POOL_HALO - k:POOL_HALO - k + tm, cols]
        cnt = jnp.minimum(t + 1, w).astype(F32)
        pooled = win / cnt - ug
        y = jnp.dot(pooled.astype(BF16), wlin_ref[gi], preferred_element_type=F32)
        ys.append(y)
    y = jnp.concatenate(ys, axis=1) * pscale_ref[...]
    y_pool = jnp.dot(y.astype(BF16), wpo_ref[...], preferred_element_type=F32)

    for d, o_ref, l_ref, nat in ((DILATIONS[1], o1_ref, l1_ref, nat1), (DILATIONS[2], o2_ref, l2_ref, nat2)):
        for r in range(d):
            dst = pl.ds(r, tm // d, stride=d)
            for h in range(HEADS_PER_GROUP):
                nat[h, dst, :] = o_ref[r, :, h * HEAD_DIM:(h + 1) * HEAD_DIM]
            nat[HEADS_PER_GROUP, dst, :] = l_ref[r]

    l0, l1, l2 = l0_ref[...], nat1[HEADS_PER_GROUP], nat2[HEADS_PER_GROUP]
    m = jnp.maximum(jnp.maximum(l0, l1), l2)
    e0, e1, e2 = jnp.exp(l0 - m), jnp.exp(l1 - m), jnp.exp(l2 - m)
    den = e0 + e1 + e2
    w0, w1, w2 = e0 / den, e1 / den, e2 / den
    parts = []
    for h in range(HEADS_PER_GROUP):
        cols = slice(h * HEAD_DIM, (h + 1) * HEAD_DIM)
        parts.append(w0[:, h:h + 1] * o0_ref[:, cols]
                     + w1[:, h:h + 1] * nat1[h]
                     + w2[:, h:h + 1] * nat2[h])
    y_att = jnp.concatenate(parts, axis=1)
    y_attn = jnp.dot(y_att.astype(BF16), wao_ref[...], preferred_element_type=F32)

    g_pool = gate_ref[:, 0:D_MODEL].astype(F32)
    g_attn = gate_ref[:, D_MODEL:].astype(F32)
    mixed = g_pool * y_pool + g_attn * y_attn
    x1 = x_ref[...] + jnp.dot(mixed.astype(BF16), wout_ref[...], preferred_element_type=F32)
    x1_ref[...] = x1
    h2_ref[...] = _rms_norm(x1, gffn_ref[...]).astype(BF16)


def _mix(u, os_, lses, gates, x, wlin, pscale, wpo, wao, wout, g_ffn):
    s = x.shape[0]
    tm = MIX_TM
    row = lambda w: pl.BlockSpec((tm, w), lambda i: (i, 0))
    const2 = lambda a: pl.BlockSpec(a.shape, lambda i: (0, 0))
    halo = pl.BlockSpec((POOL_HALO, POOL_WIDTH),
                        lambda i: (jnp.maximum(i * (tm // POOL_HALO) - 1, 0), 0))

    def dilated(d, w):
        if d == 1:
            return pl.BlockSpec((None, tm, w), lambda i: (0, i, 0))
        return pl.BlockSpec((d, tm // d, w), lambda i: (0, i, 0))

    return pl.pallas_call(
        _mix_kernel,
        grid=(s // tm,),
        in_specs=[row(POOL_WIDTH), halo,
                  *[dilated(d, GROUP_WIDTH) for d in DILATIONS],
                  *[dilated(d, HEAD_DIM) for d in DILATIONS],
                  row(GATE_WIDTH), row(D_MODEL),
                  pl.BlockSpec(wlin.shape, lambda i: (0, 0, 0)),
                  const2(pscale), const2(wpo), const2(wao), const2(wout), const2(g_ffn)],
        out_specs=[row(D_MODEL), row(D_MODEL)],
        out_shape=[jax.ShapeDtypeStruct((s, D_MODEL), F32),
                   jax.ShapeDtypeStruct((s, D_MODEL), BF16)],
        scratch_shapes=[pltpu.VMEM((POOL_HALO + tm, POOL_WIDTH), F32),
                        pltpu.VMEM((MIX_NSLAB, tm, LANES), F32),
                        pltpu.VMEM((MIX_NSLAB, tm, LANES), F32)],
        compiler_params=pltpu.CompilerParams(
            dimension_semantics=("arbitrary",),
            vmem_limit_bytes=56 * MIB,
        ),
        name="mix",
    )(u, u, *os_, *lses, gates, x, wlin, pscale, wpo, wao, wout, g_ffn)


FFN_TM = 512
FFN_TF = 512
FFN_CH = 128
FFN_RB = 64
FFN_NJ = D_FF // FFN_TF


def _gelu_exact(a):
    return 0.5 * a * (1.0 + lax.erf(a * np.float32(math.sqrt(0.5))))


def _causal_conv(src, cw, cb):
    y = cb
    for k in range(CONV_WIDTH):
        back = CONV_WIDTH - 1 - k
        rows = src if back == 0 else pltpu.roll(src, back, 0)
        y = y + cw[k:k + 1, :] * rows[SUBLANES:, :]
    return y


def _ffn_kernel(n_items, h_ref, wup_ref, cw_ref, cb_ref, wd_ref, x1_ref, gfin_ref, out_ref,
                act0, act1, acc_ref, tail_ref):
    t = pl.program_id(0)
    s1 = jnp.minimum(t, n_items - 1)
    j1 = s1 % FFN_NJ
    s2 = jnp.clip(t - 1, 0, n_items - 1)
    j2 = s2 % FFN_NJ

    @pl.when(t == 0)
    def _():
        act1[...] = jnp.zeros_like(act1)
        tail_ref[...] = jnp.zeros_like(tail_ref)

    @pl.when(j2 == 0)
    def _():
        acc_ref[...] = x1_ref[...]

    def stages(act_w, act_r):
        n_chunks = FFN_TF // FFN_CH
        for c in range(n_chunks):
            ucols = slice(2 * FFN_CH * c, 2 * FFN_CH * (c + 1))
            up = jnp.dot(h_ref[...], wup_ref[:, ucols], preferred_element_type=F32)
            prev = tail_ref[j1, :, ucols]
            tail_ref[j1, :, ucols] = up[FFN_TM - SUBLANES:, :]
            if c == n_chunks - 1:
                acc_ref[...] += jnp.dot(act_r[...], wd_ref[...], preferred_element_type=F32)
            cw_a, cw_b = cw_ref[:, ucols.start:ucols.start + FFN_CH], cw_ref[:, ucols.start + FFN_CH:ucols.stop]
            cb_a, cb_b = cb_ref[:, ucols.start:ucols.start + FFN_CH], cb_ref[:, ucols.start + FFN_CH:ucols.stop]
            for r0 in range(0, FFN_TM, FFN_RB):
                if r0 == 0:
                    src = jnp.concatenate([prev, up[0:FFN_RB]], axis=0)
                else:
                    src = up[r0 - SUBLANES:r0 + FFN_RB]
                a = _causal_conv(src[:, :FFN_CH], cw_a, cb_a)
                b = _causal_conv(src[:, FFN_CH:], cw_b, cb_b)
                act_w[r0:r0 + FFN_RB, FFN_CH * c:FFN_CH * (c + 1)] = (_gelu_exact(a) * b).astype(BF16)

    @pl.when(t % 2 == 0)
    def _():
        stages(act0, act1)

    @pl.when(t % 2 == 1)
    def _():
        stages(act1, act0)

    @pl.when((t >= 1) & (j2 == FFN_NJ - 1))
    def _():
        out_ref[...] = _rms_norm(acc_ref[...], gfin_ref[...])


def _chunk_interleave(w):
    lead = w.shape[:-1]
    w = w.reshape(*lead, 2, D_FF // FFN_CH, FFN_CH)
    return jnp.swapaxes(w, -3, -2).reshape(*lead, 2 * D_FF)


def _ffn(h2, x1, w_up_bf16, conv_w, conv_b, w_down_bf16, g_final):
    s = h2.shape[0]
    tm, tf, nj = FFN_TM, FFN_TF, FFN_NJ
    n_items = (s // tm) * nj
    last = n_items - 1
    item1 = lambda t: jnp.minimum(t, last)
    item2 = lambda t: jnp.clip(t - 1, 0, last)
    return pl.pallas_call(
        functools.partial(_ffn_kernel, n_items),
        grid=(n_items + 1,),
        in_specs=[
            pl.BlockSpec((tm, D_MODEL), lambda t: (item1(t) // nj, 0)),
            pl.BlockSpec((D_MODEL, 2 * tf), lambda t: (0, item1(t) % nj)),
            pl.BlockSpec((CONV_WIDTH, 2 * tf), lambda t: (0, item1(t) % nj)),
            pl.BlockSpec((1, 2 * tf), lambda t: (0, item1(t) % nj)),
            pl.BlockSpec((tf, D_MODEL), lambda t: (item2(t) % nj, 0)),
            pl.BlockSpec((tm, D_MODEL), lambda t: (item2(t) // nj, 0)),
            pl.BlockSpec((1, D_MODEL), lambda t: (0, 0)),
        ],
        out_specs=pl.BlockSpec((tm, D_MODEL), lambda t: (item2(t) // nj, 0)),
        out_shape=jax.ShapeDtypeStruct((s, D_MODEL), F32),
        scratch_shapes=[pltpu.VMEM((tm, tf), BF16),
                        pltpu.VMEM((tm, tf), BF16),
                        pltpu.VMEM((tm, D_MODEL), F32),
                        pltpu.VMEM((nj, SUBLANES, 2 * tf), F32)],
        compiler_params=pltpu.CompilerParams(
            dimension_semantics=("arbitrary",),
            vmem_limit_bytes=56 * MIB,
        ),
        name="ffn",
    )(h2, w_up_bf16, conv_w, conv_b, w_down_bf16, x1, g_final)


def kernel(x, g_mix, w_in, b_gate, w_pool_lin, pool_scale, w_pool_out, w_attn_out, w_out,
           g_ffn, w_up, conv_w, conv_b, w_down, g_final):
    batch, s, _ = x.shape
    depth = g_mix.shape[0]
    assert batch == 1 and depth == 1
    l = 0
    xs = x[0]
    u, *qkvs, gates = _in_proj(xs, g_mix[l][None], _in_proj_weight(w_in[l]), b_gate[l][None])
    os_, lses = [], []
    for gi, qkv_g in enumerate(qkvs):
        o, lse = _attention_group(qkv_g, gi)
        os_.append(o)
        lses.append(lse)
    x1, h2 = _mix(u, os_, lses, gates, xs,
                  w_pool_lin[l].astype(BF16), pool_scale[l][None],
                  w_pool_out[l].astype(BF16), w_attn_out[l].astype(BF16),
                  w_out[l].astype(BF16), g_ffn[l][None])
    out = _ffn(h2, x1, _chunk_interleave(w_up[l]).astype(BF16), _chunk_interleave(conv_w[l]),
               _chunk_interleave(conv_b[l])[None], w_down[l].astype(BF16), g_final[None])
    return out[None]
```

```python
import functools
import math

import jax
import jax.numpy as jnp
import numpy as np
from jax import lax
from jax.experimental import pallas as pl
from jax.experimental.pallas import tpu as pltpu

F32 = jnp.float32
BF16 = jnp.bfloat16

D_MODEL = 2048
POOL_WINDOWS = (2, 4, 8, 16)
POOL_GROUP_WIDTH = 256
POOL_WIDTH = len(POOL_WINDOWS) * POOL_GROUP_WIDTH
ATTN_GROUPS = ((128, 1), (512, 4), (2048, 16))
DILATIONS = tuple(d for _, d in ATTN_GROUPS)
N_GROUPS = len(ATTN_GROUPS)
HEADS_PER_GROUP = 4
N_ATTN_HEADS = HEADS_PER_GROUP * N_GROUPS
HEAD_DIM = 128
ATTN_WIDTH = N_ATTN_HEADS * HEAD_DIM
GROUP_WIDTH = HEADS_PER_GROUP * HEAD_DIM
GROUP_QKV_WIDTH = 3 * GROUP_WIDTH
GATE_WIDTH = 2 * D_MODEL
D_FF = 5632
CONV_WIDTH = 3
RMS_EPS = 1e-6

SPAN = 128
POOL_HALO = 16
LANES = 128
SUBLANES = 8

MIB = 1024 * 1024


def _rms_norm(x, g):
    return x * lax.rsqrt(jnp.mean(x * x, axis=-1, keepdims=True) + RMS_EPS) * g


IN_TM = 1024
IN_TN = 512
IN_NU = POOL_WIDTH // IN_TN
IN_NGRP = GROUP_QKV_WIDTH // IN_TN
IN_NG = GATE_WIDTH // IN_TN
IN_GATE0 = IN_NU + N_GROUPS * IN_NGRP


def _in_proj_kernel(x_ref, g_ref, w_ref, b_ref, u_ref, qkv0_ref, qkv1_ref, qkv2_ref, gate_ref,
                    h_ref, slab_ref):
    j = pl.program_id(1)

    @pl.when(j == 0)
    def _():
        h_ref[...] = _rms_norm(x_ref[...], g_ref[...]).astype(BF16)

    acc = jnp.dot(h_ref[...], w_ref[...], preferred_element_type=F32)

    @pl.when(j < IN_NU)
    def _():
        u_ref[...] = acc

    for gi, (d, out_ref) in enumerate(zip(DILATIONS, (qkv0_ref, qkv1_ref, qkv2_ref))):
        lo = IN_NU + gi * IN_NGRP

        @pl.when((j >= lo) & (j < lo + IN_NGRP))
        def _(d=d, out_ref=out_ref):
            if d == 1:
                out_ref[0] = acc.astype(BF16)
                return
            for c in range(IN_TN // LANES):
                slab_ref[c] = acc[:, c * LANES:(c + 1) * LANES]
            for r in range(d):
                for c in range(IN_TN // LANES):
                    rows = slab_ref[c, pl.ds(r, IN_TM // d, stride=d), :]
                    out_ref[r, :, c * LANES:(c + 1) * LANES] = rows.astype(BF16)

    @pl.when(j >= IN_GATE0)
    def _():
        gate_ref[...] = jax.nn.sigmoid(acc + b_ref[...]).astype(BF16)


def _in_proj_weight_block(j):
    jj = jnp.clip(j - IN_NU, 0, N_GROUPS * IN_NGRP - 1)
    group, which = jj // IN_NGRP, jj % IN_NGRP
    qkv_block = IN_NU + which * N_GROUPS + group
    return jnp.where((j >= IN_NU) & (j < IN_GATE0), qkv_block, j)


def _in_proj(x, g_mix, w_in_bf16, b_gate):
    s = x.shape[0]
    grid = (s // IN_TM, IN_GATE0 + IN_NG)

    def qkv_spec(gi, d):
        lo = IN_NU + gi * IN_NGRP
        return pl.BlockSpec((d, IN_TM // d, IN_TN), lambda i, j: (0, i, jnp.clip(j - lo, 0, IN_NGRP - 1)))

    gate_col = lambda i, j: jnp.clip(j - IN_GATE0, 0, IN_NG - 1)
    return pl.pallas_call(
        _in_proj_kernel,
        grid=grid,
        in_specs=[
            pl.BlockSpec((IN_TM, D_MODEL), lambda i, j: (i, 0)),
            pl.BlockSpec((1, D_MODEL), lambda i, j: (0, 0)),
            pl.BlockSpec((D_MODEL, IN_TN), lambda i, j: (0, _in_proj_weight_block(j))),
            pl.BlockSpec((1, IN_TN), lambda i, j: (0, gate_col(i, j))),
        ],
        out_specs=[
            pl.BlockSpec((IN_TM, IN_TN), lambda i, j: (i, jnp.minimum(j, IN_NU - 1))),
            *[qkv_spec(gi, d) for gi, d in enumerate(DILATIONS)],
            pl.BlockSpec((IN_TM, IN_TN), lambda i, j: (i, gate_col(i, j))),
        ],
        out_shape=[
            jax.ShapeDtypeStruct((s, POOL_WIDTH), F32),
            *[jax.ShapeDtypeStruct((d, s // d, GROUP_QKV_WIDTH), BF16) for d in DILATIONS],
            jax.ShapeDtypeStruct((s, GATE_WIDTH), BF16),
        ],
        scratch_shapes=[pltpu.VMEM((IN_TM, D_MODEL), BF16),
                        pltpu.VMEM((IN_TN // LANES, IN_TM, LANES), F32)],
        compiler_params=pltpu.CompilerParams(
            dimension_semantics=("arbitrary", "arbitrary"),
            vmem_limit_bytes=48 * MIB,
        ),
        name="in_proj",
    )(x, g_mix, w_in_bf16, b_gate)


ATTN_TQ = 512
ATTN_NSUB = ATTN_TQ // SPAN


def _alibi_bias_table(group_index, dilation):
    slopes = np.array([2.0 ** (-8.0 * (h + 1) / N_ATTN_HEADS) for h in range(N_ATTN_HEADS)], dtype=np.float32)
    slopes = slopes[group_index * HEADS_PER_GROUP:(group_index + 1) * HEADS_PER_GROUP]
    qi = np.arange(SPAN)[:, None] + SPAN
    ki = np.arange(2 * SPAN)[None, :]
    j = qi - ki
    valid = (j >= 0) & (j <= SPAN)
    bias = -slopes[:, None, None] * (j * dilation).astype(np.float32)[None]
    return np.where(valid[None], bias, -np.inf).astype(np.float32)


def _attn_kernel(q_ref, kc_ref, kp_ref, vc_ref, vp_ref, bias_ref, o_ref, lse_ref, kbuf, vbuf):
    b = pl.program_id(1)
    kbuf[0:SPAN, :] = kp_ref[...]
    kbuf[SPAN:, :] = kc_ref[...]
    vbuf[0:SPAN, :] = vp_ref[...]
    vbuf[SPAN:, :] = vc_ref[...]

    scale = HEAD_DIM ** -0.5
    lane = lax.broadcasted_iota(jnp.int32, (SPAN, HEAD_DIM), 1)
    col = lax.broadcasted_iota(jnp.int32, (SPAN, 2 * SPAN), 1)
    before_start = (b == 0) & (col < SPAN)

    for sub in range(ATTN_NSUB):
        rows = slice(sub * SPAN, (sub + 1) * SPAN)
        krows = slice(sub * SPAN, (sub + 2) * SPAN)
        lse_tile = jnp.zeros((SPAN, HEAD_DIM), F32)
        for h in range(HEADS_PER_GROUP):
            cols = slice(h * HEAD_DIM, (h + 1) * HEAD_DIM)
            q = q_ref[rows, cols]
            k = kbuf[krows, cols]
            v = vbuf[krows, cols]
            s = lax.dot_general(q, k, (((1,), (1,)), ((), ())), preferred_element_type=F32)
            s = s * scale + bias_ref[h]
            if sub == 0:
                s = jnp.where(before_start, -jnp.inf, s)
            m = jnp.max(s, axis=-1, keepdims=True)
            p = jnp.exp(s - m)
            l = jnp.sum(p, axis=-1, keepdims=True)
            o = jnp.dot(p.astype(BF16), v, preferred_element_type=F32) / l
            o_ref[rows, cols] = o
            lse_tile = jnp.where(lane == h, m + jnp.log(l), lse_tile)
        lse_ref[rows, :] = lse_tile


def _attention_group(qkv_g, group_index):
    d, L, _ = qkv_g.shape
    tq = ATTN_TQ
    assert L % tq == 0
    bias = jnp.asarray(_alibi_bias_table(group_index, d))

    def cur(blk):
        return pl.BlockSpec((None, tq, GROUP_WIDTH), lambda r, b: (r, b, blk))

    def prev(blk):
        return pl.BlockSpec((None, SPAN, GROUP_WIDTH),
                            lambda r, b: (r, jnp.maximum(b * ATTN_NSUB - 1, 0), blk))

    return pl.pallas_call(
        _attn_kernel,
        grid=(d, L // tq),
        in_specs=[cur(0), cur(1), prev(1), cur(2), prev(2),
                  pl.BlockSpec((HEADS_PER_GROUP, SPAN, 2 * SPAN), lambda r, b: (0, 0, 0))],
        out_specs=[
            pl.BlockSpec((None, tq, GROUP_WIDTH), lambda r, b: (r, b, 0)),
            pl.BlockSpec((None, tq, HEAD_DIM), lambda r, b: (r, b, 0)),
        ],
        out_shape=[
            jax.ShapeDtypeStruct((d, L, GROUP_WIDTH), F32),
            jax.ShapeDtypeStruct((d, L, HEAD_DIM), F32),
        ],
        scratch_shapes=[pltpu.VMEM((SPAN + tq, GROUP_WIDTH), BF16),
                        pltpu.VMEM((SPAN + tq, GROUP_WIDTH), BF16)],
        compiler_params=pltpu.CompilerParams(
            dimension_semantics=("arbitrary", "arbitrary"),
            vmem_limit_bytes=32 * MIB,
        ),
        name=f"attn_d{d}",
    )(qkv_g, qkv_g, qkv_g, qkv_g, qkv_g, bias)


MIX_TM = 256
MIX_NSLAB = HEADS_PER_GROUP + 1


def _mix_kernel(u_ref, uh_ref, o0_ref, o1_ref, o2_ref, l0_ref, l1_ref, l2_ref, gate_ref, x_ref,
                wlin_ref, pscale_ref, wpo_ref, wao_ref, wout_ref, gffn_ref,
                x1_ref, h2_ref, ubuf, nat1, nat2):
    i = pl.program_id(0)
    tm = MIX_TM

    ubuf[0:POOL_HALO, :] = jnp.where(i == 0, 0.0, uh_ref[...])
    ubuf[POOL_HALO:, :] = u_ref[...]
    t = i * tm + lax.broadcasted_iota(jnp.int32, (tm, 1), 0)
    ys = []
    for gi, w in enumerate(POOL_WINDOWS):
        cols = slice(gi * POOL_GROUP_WIDTH, (gi + 1) * POOL_GROUP_WIDTH)
        ug = ubuf[POOL_HALO:POOL_HALO + tm, cols]
        win = ug
        for k in range(1, w):
            win = win + ubuf[POOL_HALO - k:POOL_HALO - k + tm, cols]
        cnt = jnp.minimum(t + 1, w).astype(F32)
        pooled = win / cnt - ug
        y = jnp.dot(pooled.astype(BF16), wlin_ref[gi], preferred_element_type=F32)
        ys.append(y)
    y = jnp.concatenate(ys, axis=1) * pscale_ref[...]
    y_pool = jnp.dot(y.astype(BF16), wpo_ref[...], preferred_element_type=F32)

    for d, o_ref, l_ref, nat in ((DILATIONS[1], o1_ref, l1_ref, nat1), (DILATIONS[2], o2_ref, l2_ref, nat2)):
        for r in range(d):
            dst = pl.ds(r, tm // d, stride=d)
            for h in range(HEADS_PER_GROUP):
                nat[h, dst, :] = o_ref[r, :, h * HEAD_DIM:(h + 1) * HEAD_DIM]
            nat[HEADS_PER_GROUP, dst, :] = l_ref[r]

    l0, l1, l2 = l0_ref[...], nat1[HEADS_PER_GROUP], nat2[HEADS_PER_GROUP]
    m = jnp.maximum(jnp.maximum(l0, l1), l2)
    e0, e1, e2 = jnp.exp(l0 - m), jnp.exp(l1 - m), jnp.exp(l2 - m)
    den = e0 + e1 + e2
    w0, w1, w2 = e0 / den, e1 / den, e2 / den
    parts = []
    for h in range(HEADS_PER_GROUP):
        cols = slice(h * HEAD_DIM, (h + 1) * HEAD_DIM)
        parts.append(w0[:, h:h + 1] * o0_ref[:, cols]
                     + w1[:, h:h + 1] * nat1[h]
                     + w2[:, h:h + 1] * nat2[h])
    y_att = jnp.concatenate(parts, axis=1)
    y_attn = jnp.dot(y_att.astype(BF16), wao_ref[...], preferred_element_type=F32)

    g_pool = gate_ref[:, 0:D_MODEL].astype(F32)
    g_attn = gate_ref[:, D_MODEL:].astype(F32)
    mixed = g_pool * y_pool + g_attn * y_attn
    x1 = x_ref[...] + jnp.dot(mixed.astype(BF16), wout_ref[...], preferred_element_type=F32)
    x1_ref[...] = x1
    h2_ref[...] = _rms_norm(x1, gffn_ref[...]).astype(BF16)


def _mix(u, os_, lses, gates, x, wlin, pscale, wpo, wao, wout, g_ffn):
    s = x.shape[0]
    tm = MIX_TM
    row = lambda w: pl.BlockSpec((tm, w), lambda i: (i, 0))
    const2 = lambda a: pl.BlockSpec(a.shape, lambda i: (0, 0))
    halo = pl.BlockSpec((POOL_HALO, POOL_WIDTH),
                        lambda i: (jnp.maximum(i * (tm // POOL_HALO) - 1, 0), 0))

    def dilated(d, w):
        if d == 1:
            return pl.BlockSpec((None, tm, w), lambda i: (0, i, 0))
        return pl.BlockSpec((d, tm // d, w), lambda i: (0, i, 0))

    return pl.pallas_call(
        _mix_kernel,
        grid=(s // tm,),
        in_specs=[row(POOL_WIDTH), halo,
                  *[dilated(d, GROUP_WIDTH) for d in DILATIONS],
                  *[dilated(d, HEAD_DIM) for d in DILATIONS],
                  row(GATE_WIDTH), row(D_MODEL),
                  pl.BlockSpec(wlin.shape, lambda i: (0, 0, 0)),
                  const2(pscale), const2(wpo), const2(wao), const2(wout), const2(g_ffn)],
        out_specs=[row(D_MODEL), row(D_MODEL)],
        out_shape=[jax.ShapeDtypeStruct((s, D_MODEL), F32),
                   jax.ShapeDtypeStruct((s, D_MODEL), BF16)],
        scratch_shapes=[pltpu.VMEM((POOL_HALO + tm, POOL_WIDTH), F32),
                        pltpu.VMEM((MIX_NSLAB, tm, LANES), F32),
                        pltpu.VMEM((MIX_NSLAB, tm, LANES), F32)],
        compiler_params=pltpu.CompilerParams(
            dimension_semantics=("arbitrary",),
            vmem_limit_bytes=56 * MIB,
        ),
        name="mix",
    )(u, u, *os_, *lses, gates, x, wlin, pscale, wpo, wao, wout, g_ffn)


FFN_TM = 512
FFN_TF = 512
FFN_CH = 256
FFN_RB = 64
FFN_NJ = D_FF // FFN_TF


def _gelu_exact(a):
    return 0.5 * a * (1.0 + lax.erf(a * np.float32(math.sqrt(0.5))))


def _causal_conv(src, cw, cb):
    y = cb
    for k in range(CONV_WIDTH):
        back = CONV_WIDTH - 1 - k
        rows = src if back == 0 else pltpu.roll(src, back, 0)
        y = y + cw[k:k + 1, :] * rows[SUBLANES:, :]
    return y


def _ffn_kernel(n_items, h_ref, wa_ref, wb_ref, cwa_ref, cwb_ref, cba_ref, cbb_ref, wd_ref, x1_ref, gfin_ref,
                out_ref, act0, act1, acc_ref, tail_a, tail_b):
    t = pl.program_id(0)
    s1 = jnp.minimum(t, n_items - 1)
    j1 = s1 % FFN_NJ
    s2 = jnp.clip(t - 1, 0, n_items - 1)
    j2 = s2 % FFN_NJ

    @pl.when(t == 0)
    def _():
        act1[...] = jnp.zeros_like(act1)
        tail_a[...] = jnp.zeros_like(tail_a)
        tail_b[...] = jnp.zeros_like(tail_b)

    @pl.when(j2 == 0)
    def _():
        acc_ref[...] = x1_ref[...]

    def stages(act_w, act_r):
        n_chunks = FFN_TF // FFN_CH
        for c in range(n_chunks):
            cols = slice(FFN_CH * c, FFN_CH * (c + 1))
            up_a = jnp.dot(h_ref[...], wa_ref[:, cols], preferred_element_type=F32)
            up_b = jnp.dot(h_ref[...], wb_ref[:, cols], preferred_element_type=F32)
            prev_a, prev_b = tail_a[j1, :, cols], tail_b[j1, :, cols]
            tail_a[j1, :, cols] = up_a[FFN_TM - SUBLANES:, :]
            tail_b[j1, :, cols] = up_b[FFN_TM - SUBLANES:, :]
            if c == n_chunks - 1:
                acc_ref[...] += jnp.dot(act_r[...], wd_ref[...], preferred_element_type=F32)
            for r0 in range(0, FFN_TM, FFN_RB):
                if r0 == 0:
                    src_a = jnp.concatenate([prev_a, up_a[0:FFN_RB]], axis=0)
                    src_b = jnp.concatenate([prev_b, up_b[0:FFN_RB]], axis=0)
                else:
                    src_a = up_a[r0 - SUBLANES:r0 + FFN_RB]
                    src_b = up_b[r0 - SUBLANES:r0 + FFN_RB]
                a = _causal_conv(src_a, cwa_ref[:, cols], cba_ref[:, cols])
                b = _causal_conv(src_b, cwb_ref[:, cols], cbb_ref[:, cols])
                act_w[r0:r0 + FFN_RB, cols] = (_gelu_exact(a) * b).astype(BF16)

    @pl.when(t % 2 == 0)
    def _():
        stages(act0, act1)

    @pl.when(t % 2 == 1)
    def _():
        stages(act1, act0)

    @pl.when((t >= 1) & (j2 == FFN_NJ - 1))
    def _():
        out_ref[...] = _rms_norm(acc_ref[...], gfin_ref[...])


def _ffn(h2, x1, w_up_bf16, conv_w, conv_b, w_down_bf16, g_final):
    s = h2.shape[0]
    tm, tf, nj = FFN_TM, FFN_TF, FFN_NJ
    n_items = (s // tm) * nj
    last = n_items - 1
    item1 = lambda t: jnp.minimum(t, last)
    item2 = lambda t: jnp.clip(t - 1, 0, last)
    return pl.pallas_call(
        functools.partial(_ffn_kernel, n_items),
        grid=(n_items + 1,),
        in_specs=[
            pl.BlockSpec((tm, D_MODEL), lambda t: (item1(t) // nj, 0)),
            pl.BlockSpec((D_MODEL, tf), lambda t: (0, item1(t) % nj)),
            pl.BlockSpec((D_MODEL, tf), lambda t: (0, item1(t) % nj + nj)),
            pl.BlockSpec((CONV_WIDTH, tf), lambda t: (0, item1(t) % nj)),
            pl.BlockSpec((CONV_WIDTH, tf), lambda t: (0, item1(t) % nj + nj)),
            pl.BlockSpec((1, tf), lambda t: (0, item1(t) % nj)),
            pl.BlockSpec((1, tf), lambda t: (0, item1(t) % nj + nj)),
            pl.BlockSpec((tf, D_MODEL), lambda t: (item2(t) % nj, 0)),
            pl.BlockSpec((tm, D_MODEL), lambda t: (item2(t) // nj, 0)),
            pl.BlockSpec((1, D_MODEL), lambda t: (0, 0)),
        ],
        out_specs=pl.BlockSpec((tm, D_MODEL), lambda t: (item2(t) // nj, 0)),
        out_shape=jax.ShapeDtypeStruct((s, D_MODEL), F32),
        scratch_shapes=[pltpu.VMEM((tm, tf), BF16),
                        pltpu.VMEM((tm, tf), BF16),
                        pltpu.VMEM((tm, D_MODEL), F32),
                        pltpu.VMEM((nj, SUBLANES, tf), F32),
                        pltpu.VMEM((nj, SUBLANES, tf), F32)],
        compiler_params=pltpu.CompilerParams(
            dimension_semantics=("arbitrary",),
            vmem_limit_bytes=56 * MIB,
        ),
        name="ffn",
    )(h2, w_up_bf16, w_up_bf16, conv_w, conv_w, conv_b, conv_b, w_down_bf16, x1, g_final)


def kernel(x, g_mix, w_in, b_gate, w_pool_lin, pool_scale, w_pool_out, w_attn_out, w_out,
           g_ffn, w_up, conv_w, conv_b, w_down, g_final):
    batch, s, _ = x.shape
    depth = g_mix.shape[0]
    assert batch == 1 and depth == 1
    l = 0
    xs = x[0]
    u, *qkvs, gates = _in_proj(xs, g_mix[l][None], w_in[l].astype(BF16), b_gate[l][None])
    os_, lses = [], []
    for gi, qkv_g in enumerate(qkvs):
        o, lse = _attention_group(qkv_g, gi)
        os_.append(o)
        lses.append(lse)
    x1, h2 = _mix(u, os_, lses, gates, xs,
                  w_pool_lin[l].astype(BF16), pool_scale[l][None],
                  w_pool_out[l].astype(BF16), w_attn_out[l].astype(BF16),
                  w_out[l].astype(BF16), g_ffn[l][None])
    out = _ffn(h2, x1, w_up[l].astype(BF16), conv_w[l], conv_b[l][None], w_down[l].astype(BF16), g_final[None])
    return out[None]
```
